```python
import math
import jax, jax.numpy as jnp
from jax import lax
import numpy as np

D_MODEL = 1024
BATCH = 16
SEQ = 256
DEPTH = 2
DEC_BATCH = 4
DEC_SEQ = 1024
PAST_LEN = 256

GRID_W = 64
D_HYENA = 256
D_SCONV = 256
N_HEADS = 8
QK_NOPE = 64
QK_ROPE = 32
QK_DIM = QK_NOPE + QK_ROPE
V_DIM = 64
D_MLA = N_HEADS * V_DIM
D_MIX = D_HYENA + D_SCONV + D_MLA
Q_LORA = 256
KV_LORA = 128
ROPE_THETA = 10000.0
HY_EMB = 33
HY_BANDS = (HY_EMB - 1) // 2
HY_FF = 64
HY_FAST_DECAY = 0.3
HY_SLOW_DECAY = 1.5
HY_TARGET = 1e-2
HY_SHIFT = 0.05
D_FF = ((8 * D_MODEL) // 3 + 255) // 256 * 256
N_IN = 3 * D_HYENA + 3 * D_SCONV + Q_LORA + KV_LORA + QK_ROPE
EPS = 1e-6
Q_BLOCK = 128

kernel_name = 'hybrid_hyena_sconv_mla_diffusion_step'


def rmsnorm(x, g):
    xf = x.astype(jnp.float32)
    y = xf * lax.rsqrt(jnp.mean(xf * xf, axis=-1, keepdims=True) + EPS)
    return (y * g.astype(jnp.float32)).astype(x.dtype)


def conv3(x, w):
    xp = jnp.pad(x, ((0, 0), (1, 1), (0, 0)))
    return xp[:, :-2] * w[0] + xp[:, 1:-1] * w[1] + xp[:, 2:] * w[2]


def axial_rope_tables(L):
    rows = L // GRID_W
    row = jnp.repeat(jnp.arange(rows, dtype=jnp.float32), GRID_W)
    col = jnp.tile(jnp.arange(GRID_W, dtype=jnp.float32), rows)
    half = QK_ROPE // 2
    inv = 1.0 / (ROPE_THETA ** (jnp.arange(0, half, 2, dtype=jnp.float32) / half))
    ang = jnp.concatenate([row[:, None] * inv, col[:, None] * inv], axis=-1)
    return jnp.cos(ang), jnp.sin(ang)


def apply_rope(x, cos, sin):
    Bn, L, H, _ = x.shape
    q4 = QK_ROPE // 4
    xf = x.astype(jnp.float32)
    pe = xf[..., QK_NOPE:].reshape(Bn, L, H, 2, 2, q4)
    c = cos.reshape(L, 2, q4)[None, :, None]
    s = sin.reshape(L, 2, q4)[None, :, None]
    a, b = pe[..., 0, :], pe[..., 1, :]
    rot = jnp.stack([a * c - b * s, b * c + a * s], axis=-2).reshape(Bn, L, H, QK_ROPE)
    return jnp.concatenate([xf[..., :QK_NOPE], rot], axis=-1).astype(x.dtype)


def hyena_filter_fft(L, w1, b1, freq, w2, b2, w3):
    f32 = jnp.float32
    t = jnp.linspace(0.0, 1.0, L, dtype=f32)[:, None]
    w_ang = 2.0 * math.pi * jnp.arange(L, dtype=f32)[:, None] / L
    bands = jnp.linspace(1e-4, HY_BANDS - 1, HY_BANDS, dtype=f32)[None, :]
    z = jnp.concatenate([t, jnp.cos(bands * w_ang), -jnp.sin(bands * w_ang)], axis=-1)
    fr = freq.astype(f32)
    h = jnp.sin(fr * (z @ w1.astype(f32) + b1.astype(f32)))
    h = jnp.sin(fr * (h @ w2.astype(f32) + b2.astype(f32)))
    h = h @ w3.astype(f32)
    deltas = jnp.abs(jnp.linspace(math.log(HY_TARGET) / HY_FAST_DECAY,
                                  math.log(HY_TARGET) / HY_SLOW_DECAY, D_HYENA, dtype=f32))
    window = jnp.exp(-t * deltas[None, :]) + HY_SHIFT
    h_f = h[:, :D_HYENA] * window
    h_b = h[:, D_HYENA:] * window
    k = jnp.concatenate([h_f, jnp.zeros((1, D_HYENA), f32), h_b[:0:-1]], axis=0)
    return jnp.fft.rfft(k, axis=0)


def hyena_mixer(u, conv_w, w1, b1, freq, w2, b2, w3, bias):
    L = u.shape[1]
    u = conv3(u, conv_w)
    x0, x1, v = jnp.split(u, 3, axis=-1)
    kf = hyena_filter_fft(L, w1, b1, freq, w2, b2, w3)
    z = (x1 * v).astype(jnp.float32)
    Z = jnp.fft.rfft(z, n=2 * L, axis=1)
    y = jnp.fft.irfft(Z * kf[None], n=2 * L, axis=1)[:, :L]
    y = y + z * bias.astype(jnp.float32)
    return (x0.astype(jnp.float32) * y).astype(u.dtype)


def sconv_mixer(u, conv_w):
    bg, cg, hx = jnp.split(u, 3, axis=-1)
    return bg * conv3(cg * hx, conv_w)


def mla_q(c_q, g_q, w_uq, g_qh):
    Bn, L, _ = c_q.shape
    q = (rmsnorm(c_q, g_q) @ w_uq).reshape(Bn, L, N_HEADS, QK_DIM)
    return rmsnorm(q, g_qh)


def mla_kv(c_kv, k_pe, g_kv, w_ukv, g_kh):
    Bn, L, _ = c_kv.shape
    kv = (rmsnorm(c_kv, g_kv) @ w_ukv).reshape(Bn, L, N_HEADS, QK_NOPE + V_DIM)
    k_nope, v = kv[..., :QK_NOPE], kv[..., QK_NOPE:]
    k = jnp.concatenate([k_nope, jnp.broadcast_to(k_pe[:, :, None, :], (Bn, L, N_HEADS, QK_ROPE))], axis=-1)
    return rmsnorm(k, g_kh), v


def attention(q, k, v):
    Bn, Lq, H, Dq = q.shape
    nb = Lq // Q_BLOCK
    kf = k.astype(jnp.float32)
    vf = v.astype(jnp.float32)
    qb = q.astype(jnp.float32).reshape(Bn, nb, Q_BLOCK, H, Dq).swapaxes(0, 1)
    scale = QK_DIM ** -0.5

    def one_block(qi):
        s = jnp.einsum('bqhd,bkhd->bhqk', qi, kf) * scale
        p = jax.nn.softmax(s, axis=-1)
        return jnp.einsum('bhqk,bkhd->bqhd', p, vf)

    o = lax.map(one_block, qb).swapaxes(0, 1).reshape(Bn, Lq, H * V_DIM)
    return o.astype(v.dtype)


def trunk_layer(x, mod, P, l, ctx, rope):
    sh1, sc1, g1, sh2, sc2, g2 = jnp.split(mod, 6, axis=-1)
    h = rmsnorm(x, P['g_norm1'][l]) * (1 + sc1) + sh1
    proj = h @ P['w_in'][l]
    o0 = 3 * D_HYENA
    o1 = o0 + 3 * D_SCONV
    o2 = o1 + Q_LORA
    o3 = o2 + KV_LORA
    u_hy, u_sc = proj[..., :o0], proj[..., o0:o1]
    c_q, c_kv, k_pe = proj[..., o1:o2], proj[..., o2:o3], proj[..., o3:]

    y_hy = hyena_mixer(u_hy, P['hy_conv'][l], P['hy_w1'][l], P['hy_b1'][l], P['hy_freq'][l],
                       P['hy_w2'][l], P['hy_b2'][l], P['hy_w3'][l], P['hy_bias'][l])
    y_sc = sconv_mixer(u_sc, P['sc_conv'][l])

    q = mla_q(c_q, P['g_q'][l], P['w_uq'][l], P['g_qh'][l])
    k, v = mla_kv(c_kv, k_pe, P['g_kv'][l], P['w_ukv'][l], P['g_kh'][l])
    if rope is not None:
        q = apply_rope(q, rope[0], rope[1])
        k = apply_rope(k, rope[0], rope[1])
    if ctx is not None:
        ck, cv = mla_kv(ctx[0], ctx[1], P['g_kv'][l], P['w_ukv'][l], P['g_kh'][l])
        k = jnp.concatenate([k, ck], axis=1)
        v = jnp.concatenate([v, cv], axis=1)
    y_at = attention(q, k, v)

    gg = P['g_grp'][l]
    y = jnp.concatenate([rmsnorm(y_hy, gg[:D_HYENA]),
                         rmsnorm(y_sc, gg[D_HYENA:D_HYENA + D_SCONV]),
                         rmsnorm(y_at, gg[D_HYENA + D_SCONV:])], axis=-1)
    x = x + g1 * (y @ P['w_out'][l])

    h2 = rmsnorm(x, P['g_norm2'][l]) * (1 + sc2) + sh2
    ff = (jax.nn.silu(h2 @ P['w_ff_gate'][l]) * (h2 @ P['w_ff_up'][l])) @ P['w_ff_down'][l]
    x = x + g2 * ff
    return x, c_kv, k_pe


def setup_inputs(seed: int = 0) -> dict:
    key = jax.random.key(seed)
    ks = iter(jax.random.split(key, 40))
    f32 = jnp.float32

    def nrm(shape, scale):
        return jax.random.normal(next(ks), shape, f32) * scale

    def gain(shape):
        return 1.0 + nrm(shape, 0.01)

    return {
        'x_prompt': nrm((BATCH, SEQ, D_MODEL), 1.0),
        'x_sample': nrm((DEC_BATCH, DEC_SEQ, D_MODEL), 1.0),
        'cache_ckv': nrm((DEC_BATCH, DEPTH, PAST_LEN, KV_LORA), 1.0),
        'cache_kpe': nrm((DEC_BATCH, DEPTH, PAST_LEN, QK_ROPE), 1.0),
        'c': nrm((DEC_BATCH, D_MODEL), 1.0),
        'c_ctx': nrm((D_MODEL,), 1.0),
        'w_mod': nrm((DEPTH, D_MODEL, 6 * D_MODEL), 0.3 * D_MODEL ** -0.5),
        'b_mod': nrm((DEPTH, 6 * D_MODEL), 0.01),
        'g_norm1': gain((DEPTH, D_MODEL)),
        'w_in': nrm((DEPTH, D_MODEL, N_IN), D_MODEL ** -0.5),
        'hy_conv': nrm((DEPTH, 3, 3 * D_HYENA), 3 ** -0.5),
        'hy_w1': nrm((DEPTH, HY_EMB, HY_FF), HY_EMB ** -0.5),
        'hy_b1': nrm((DEPTH, HY_FF), 0.01),
        'hy_freq': 1.0 + nrm((DEPTH, HY_FF), 0.1),
        'hy_w2': nrm((DEPTH, HY_FF, HY_FF), HY_FF ** -0.5),
        'hy_b2': nrm((DEPTH, HY_FF), 0.01),
        'hy_w3': nrm((DEPTH, HY_FF, 2 * D_HYENA), HY_FF ** -0.5),
        'hy_bias': nrm((DEPTH, D_HYENA), 0.5),
        'sc_conv': nrm((DEPTH, 3, D_SCONV), 3 ** -0.5),
        'g_q': gain((DEPTH, Q_LORA)),
        'w_uq': nrm((DEPTH, Q_LORA, N_HEADS * QK_DIM), Q_LORA ** -0.5),
        'g_kv': gain((DEPTH, KV_LORA)),
        'w_ukv': nrm((DEPTH, KV_LORA, N_HEADS * (QK_NOPE + V_DIM)), KV_LORA ** -0.5),
        'g_qh': gain((DEPTH, QK_DIM)),
        'g_kh': gain((DEPTH, QK_DIM)),
        'g_grp': gain((DEPTH, D_MIX)),
        'w_out': nrm((DEPTH, D_MIX, D_MODEL), D_MIX ** -0.5),
        'g_norm2': gain((DEPTH, D_MODEL)),
        'w_ff_gate': nrm((DEPTH, D_MODEL, D_FF), D_MODEL ** -0.5),
        'w_ff_up': nrm((DEPTH, D_MODEL, D_FF), D_MODEL ** -0.5),
        'w_ff_down': nrm((DEPTH, D_FF, D_MODEL), D_FF ** -0.5),
    }


def reference(x_prompt, x_sample, cache_ckv, cache_kpe, c, c_ctx, w_mod, b_mod, g_norm1, w_in,
              hy_conv, hy_w1, hy_b1, hy_freq, hy_w2, hy_b2, hy_w3, hy_bias, sc_conv,
              g_q, w_uq, g_kv, w_ukv, g_qh, g_kh, g_grp, w_out, g_norm2,
              w_ff_gate, w_ff_up, w_ff_down):
    P = {'g_norm1': g_norm1, 'w_in': w_in, 'hy_conv': hy_conv, 'hy_w1': hy_w1, 'hy_b1': hy_b1,
         'hy_freq': hy_freq, 'hy_w2': hy_w2, 'hy_b2': hy_b2, 'hy_w3': hy_w3, 'hy_bias': hy_bias,
         'sc_conv': sc_conv, 'g_q': g_q, 'w_uq': w_uq, 'g_kv': g_kv, 'w_ukv': w_ukv,
         'g_qh': g_qh, 'g_kh': g_kh, 'g_grp': g_grp, 'w_out': w_out, 'g_norm2': g_norm2,
         'w_ff_gate': w_ff_gate, 'w_ff_up': w_ff_up, 'w_ff_down': w_ff_down}

    xp = x_prompt
    ckv_list = []
    kpe_list = []
    for l in range(DEPTH):
        mod = (jax.nn.silu(c_ctx) @ w_mod[l] + b_mod[l])[None, None, :]
        xp, ckv, kpe = trunk_layer(xp, mod, P, l, None, None)
        ckv_list.append(ckv)
        kpe_list.append(kpe)
    new_ckv = jnp.stack(ckv_list, axis=1)
    new_kpe = jnp.stack(kpe_list, axis=1)

    xs = x_sample
    rope = axial_rope_tables(x_sample.shape[1])
    for l in range(DEPTH):
        mod = (jax.nn.silu(c) @ w_mod[l] + b_mod[l])[:, None, :]
        xs, _, _ = trunk_layer(xs, mod, P, l, (cache_ckv[:, l], cache_kpe[:, l]), rope)

    return (xp, xs, new_ckv, new_kpe)
```

```python
import functools
import math

import numpy as np
import jax
import jax.numpy as jnp
from jax import lax
from jax.experimental import pallas as pl
from jax.experimental.pallas import tpu as pltpu

D_MODEL = 1024
BATCH = 16
SEQ = 256
DEPTH = 2
DEC_BATCH = 4
DEC_SEQ = 1024
PAST_LEN = 256
GRID_W = 64
D_HYENA = 256
D_SCONV = 256
N_HEADS = 8
QK_NOPE = 64
QK_ROPE = 32
QK_DIM = QK_NOPE + QK_ROPE
V_DIM = 64
D_MLA = N_HEADS * V_DIM
D_MIX = D_HYENA + D_SCONV + D_MLA
Q_LORA = 256
KV_LORA = 128
ROPE_THETA = 10000.0
HY_EMB = 33
HY_BANDS = (HY_EMB - 1) // 2
HY_FF = 64
HY_FAST_DECAY = 0.3
HY_SLOW_DECAY = 1.5
HY_TARGET = 1e-2
HY_SHIFT = 0.05
D_FF = ((8 * D_MODEL) // 3 + 255) // 256 * 256
N_IN = 3 * D_HYENA + 3 * D_SCONV + Q_LORA + KV_LORA + QK_ROPE
EPS = 1e-6

LANES = 128
HEAD_PAD = LANES
N_IN_PAD = 3 * D_HYENA + 3 * D_SCONV + Q_LORA + KV_LORA + LANES
N_CTX_TOK = BATCH * SEQ
N_DEC_TOK = DEC_BATCH * DEC_SEQ
N_TOK = N_CTX_TOK + N_DEC_TOK
MOD_ROWS = 8
TOK_TILE = 512
Q_TILE = 256
MOD_COL_TILE = 1536
VMEM_LIMIT = 56 * 1024 * 1024

F32 = jnp.float32
BF16 = jnp.bfloat16
HIGHEST = lax.Precision.HIGHEST


def _params(sem):
    return pltpu.CompilerParams(dimension_semantics=sem, vmem_limit_bytes=VMEM_LIMIT)


def _resident(shape, index_map):
    return pl.BlockSpec(shape, index_map, pipeline_mode=pl.Buffered(1))


def _rms(x, n):
    return lax.rsqrt(jnp.sum(x * x, axis=-1, keepdims=True) * (1.0 / n) + EPS)


def _mod_kernel(c_ref, w_ref, b_ref, o_ref):
    c = c_ref[...]
    a = c * jax.nn.sigmoid(c)
    o_ref[...] = jnp.dot(a, w_ref[...], preferred_element_type=F32) + b_ref[...]


def _mod_call(cs, w_mod, b_mod):
    n = 6 * D_MODEL
    return pl.pallas_call(
        _mod_kernel,
        grid=(DEPTH, n // MOD_COL_TILE),
        in_specs=[
            pl.BlockSpec((MOD_ROWS, D_MODEL), lambda l, j: (0, 0)),
            pl.BlockSpec((None, D_MODEL, MOD_COL_TILE), lambda l, j: (l, 0, j)),
            pl.BlockSpec((None, 1, MOD_COL_TILE), lambda l, j: (l, 0, j)),
        ],
        out_specs=pl.BlockSpec((None, MOD_ROWS, MOD_COL_TILE), lambda l, j: (l, 0, j)),
        out_shape=jax.ShapeDtypeStruct((DEPTH, MOD_ROWS, n), F32),
        compiler_params=_params(("arbitrary", "arbitrary")),
        name="mod",
    )(cs, w_mod, b_mod.reshape(DEPTH, 1, n))


def _mod_row(i):
    ctx_tiles = N_CTX_TOK // TOK_TILE
    per_seq = DEC_SEQ // TOK_TILE
    return jnp.where(i < ctx_tiles, 0, (i - ctx_tiles) // per_seq + 1)


def _filter_kernel(z_ref, win_ref, cos_ref, sin_ref, w1_ref, b1_ref, fr_ref, w2_ref, b2_ref,
                   w3_ref, kr_ref, ki_ref):
    fr = fr_ref[...]
    h = jnp.sin(fr * (jnp.dot(z_ref[...], w1_ref[...], precision=HIGHEST,
                              preferred_element_type=F32) + b1_ref[...]))
    h = jnp.sin(fr * (jnp.dot(h, w2_ref[...], precision=HIGHEST,
                              preferred_element_type=F32) + b2_ref[...]))
    h = jnp.dot(h, w3_ref[...], precision=HIGHEST, preferred_element_type=F32)
    win = win_ref[...]
    row = lax.broadcasted_iota(jnp.int32, win.shape, 0)
    h_f = h[:, :D_HYENA] * win
    h_b = jnp.where(row == 0, 0.0, h[:, D_HYENA:] * win)
    even = h_f + h_b
    odd = h_b - h_f
    kr_ref[...] = jnp.dot(cos_ref[...], even, precision=HIGHEST, preferred_element_type=F32)
    nyq = jnp.sum(jnp.where(row % 2 == 0, even, -even), axis=0, keepdims=True)
    ki = jnp.dot(sin_ref[...], odd, precision=HIGHEST, preferred_element_type=F32)
    ki_ref[...] = jnp.where(row == 0, nyq, ki)


def _filter_call(L, z, win, cos_t, sin_t, w1, b1, fr, w2, b2, w3):
    full = lambda shape: pl.BlockSpec(shape, lambda l: (0,) * len(shape))
    per_layer = lambda r, c: pl.BlockSpec((None, r, c), lambda l: (l, 0, 0))
    return pl.pallas_call(
        _filter_kernel,
        grid=(DEPTH,),
        in_specs=[full((L, LANES)), full((L, D_HYENA)), full((L, L)), full((L, L)),
                  per_layer(LANES, LANES), per_layer(1, LANES), per_layer(1, LANES),
                  per_layer(LANES, LANES), per_layer(1, LANES), per_layer(LANES, 2 * D_HYENA)],
        out_specs=[per_layer(L, D_HYENA), per_layer(L, D_HYENA)],
        out_shape=[jax.ShapeDtypeStruct((DEPTH, L, D_HYENA), F32)] * 2,
        compiler_params=_params(("arbitrary",)),
        name=f"hyena_filter_{L}",
    )(z, win, cos_t, sin_t, w1, b1, fr, w2, b2, w3)


def _dft_tables(L):
    f = np.arange(L, dtype=np.int64)[:, None]
    s = np.arange(L, dtype=np.int64)[None, :]
    ang = np.pi * ((f * s) % (2 * L)).astype(np.float64) / L
    cos_t = np.cos(ang)
    sin_t = np.sin(ang)
    alt = np.where(np.arange(L) % 2 == 0, 1.0, -1.0)
    sin_fwd = sin_t.copy()
    sin_fwd[0, :] = alt
    fwd = np.concatenate([cos_t, sin_fwd], axis=0)
    wr = np.full((L,), 2.0)
    wr[0] = 1.0
    inv_r = cos_t.T * wr[None, :] / (2 * L)
    inv_i = -sin_t.T * 2.0 / (2 * L)
    inv_i[:, 0] = alt / (2 * L)
    inv = np.concatenate([inv_r, inv_i], axis=1)
    return (jnp.asarray(cos_t, F32), jnp.asarray(sin_t, F32),
            jnp.asarray(fwd, F32).astype(BF16), jnp.asarray(inv, F32).astype(BF16))


def _filter_consts(L):
    t = jnp.linspace(0.0, 1.0, L, dtype=F32)[:, None]
    w_ang = 2.0 * math.pi * jnp.arange(L, dtype=F32)[:, None] / L
    bands = jnp.linspace(1e-4, HY_BANDS - 1, HY_BANDS, dtype=F32)[None, :]
    z = jnp.concatenate([t, jnp.cos(bands * w_ang), -jnp.sin(bands * w_ang)], axis=-1)
    z = jnp.pad(z, ((0, 0), (0, LANES - HY_EMB)))
    deltas = jnp.abs(jnp.linspace(math.log(HY_TARGET) / HY_FAST_DECAY,
                                  math.log(HY_TARGET) / HY_SLOW_DECAY, D_HYENA, dtype=F32))
    window = jnp.exp(-t * deltas[None, :]) + HY_SHIFT
    return z, window


def _rope_tables(L):
    rows = L // GRID_W
    row = jnp.repeat(jnp.arange(rows, dtype=F32), GRID_W)
    col = jnp.tile(jnp.arange(GRID_W, dtype=F32), rows)
    half = QK_ROPE // 2
    inv = 1.0 / (ROPE_THETA ** (jnp.arange(0, half, 2, dtype=F32) / half))
    ang = jnp.concatenate([row[:, None] * inv, col[:, None] * inv], axis=-1)
    cos, sin = jnp.cos(ang), jnp.sin(ang)
    q4 = QK_ROPE // 4
    c2 = cos.reshape(L, 2, 1, q4)
    s2 = sin.reshape(L, 2, 1, q4)
    cos32 = jnp.broadcast_to(c2, (L, 2, 2, q4)).reshape(L, QK_ROPE)
    sin32 = jnp.concatenate([-s2, s2], axis=2).reshape(L, QK_ROPE)
    ones = jnp.ones((L, QK_NOPE), F32)
    tail = HEAD_PAD - QK_DIM
    cos_t = jnp.concatenate([ones, cos32, jnp.ones((L, tail), F32)], axis=-1)
    sin_t = jnp.concatenate([0 * ones, sin32, jnp.zeros((L, tail), F32)], axis=-1)
    return cos_t, sin_t


def _inproj_kernel(x_ref, mod_ref, g_ref, w_ref, uhy_ref, usc_ref, cq_ref, ckv_ref, kpe_ref):
    x = x_ref[...]
    xn = x * _rms(x, D_MODEL) * g_ref[...]
    sh = mod_ref[:, 0:D_MODEL]
    sc = mod_ref[:, D_MODEL:2 * D_MODEL]
    h = (xn * (1.0 + sc) + sh).astype(BF16)
    p = jnp.dot(h, w_ref[...], preferred_element_type=F32)
    o0 = 3 * D_HYENA
    o1 = o0 + 3 * D_SCONV
    o2 = o1 + Q_LORA
    o3 = o2 + KV_LORA
    uhy_ref[...] = p[:, :o0].astype(BF16)
    usc_ref[...] = p[:, o0:o1].astype(BF16)
    cq_ref[...] = p[:, o1:o2].astype(BF16)
    ckv_ref[...] = p[:, o2:o3]
    kpe_ref[...] = p[:, o3:]


def _inproj_call(l, x, mod4, g_norm1, w_in_p):
    tok = lambda c: pl.BlockSpec((TOK_TILE, c), lambda i: (i, 0))
    return pl.pallas_call(
        _inproj_kernel,
        grid=(N_TOK // TOK_TILE,),
        in_specs=[
            tok(D_MODEL),
            pl.BlockSpec((None, None, 1, 6 * D_MODEL), lambda i: (l, _mod_row(i), 0, 0)),
            pl.BlockSpec((None, 1, D_MODEL), lambda i: (l, 0, 0)),
            _resident((None, D_MODEL, N_IN_PAD), lambda i: (l, 0, 0)),
        ],
        out_specs=[tok(3 * D_HYENA), tok(3 * D_SCONV), tok(Q_LORA), tok(KV_LORA), tok(LANES)],
        out_shape=[
            jax.ShapeDtypeStruct((N_TOK, 3 * D_HYENA), BF16),
            jax.ShapeDtypeStruct((N_TOK, 3 * D_SCONV), BF16),
            jax.ShapeDtypeStruct((N_TOK, Q_LORA), BF16),
            jax.ShapeDtypeStruct((N_TOK, KV_LORA), F32),
            jax.ShapeDtypeStruct((N_TOK, LANES), F32),
        ],
        compiler_params=_params(("parallel",)),
        name=f"inproj_{l}",
    )(x, mod4, g_norm1.reshape(DEPTH, 1, D_MODEL), w_in_p)


def _shift_down(x):
    row = lax.broadcasted_iota(jnp.int32, x.shape, 0)
    return jnp.where(row == 0, 0.0, pltpu.roll(x, 1, axis=0))


def _shift_up(x):
    n = x.shape[0]
    row = lax.broadcasted_iota(jnp.int32, x.shape, 0)
    return jnp.where(row == n - 1, 0.0, pltpu.roll(x, n - 1, axis=0))


def _conv3(x, w):
    return _shift_down(x) * w[0:1, :] + x * w[1:2, :] + _shift_up(x) * w[2:3, :]


def _rope(x, cos, sin):
    lane = lax.broadcasted_iota(jnp.int32, x.shape, 1)
    first = (lane % (QK_ROPE // 2)) < (QK_ROPE // 4)
    partner = jnp.where(first, pltpu.roll(x, HEAD_PAD - QK_ROPE // 4, axis=1),
                        pltpu.roll(x, QK_ROPE // 4, axis=1))
    return x * cos + partner * sin


def _mixer_kernel(*refs, L, n_ctx, rope):
    it = iter(refs)
    uhy_ref, usc_ref, cq_ref, ckv_ref, kpe_ref = (next(it) for _ in range(5))
    if n_ctx:
        cckv_ref, ckpe_ref = next(it), next(it)
    if rope:
        cos_ref, sin_ref = next(it), next(it)
    (fwd_ref, inv_ref, kr_ref, ki_ref, hyconv_ref, hybias_ref, scconv_ref, gq_ref, wq_ref,
     gkv_ref, wk_ref, wv_ref, gqh_ref, gkh_ref, ggrp_ref) = (next(it) for _ in range(15))
    out_ref = next(it)
    q_s, k_s, vl_s, vr_s, yat_s = (next(it) for _ in range(5))

    u = _conv3(uhy_ref[...].astype(F32), hyconv_ref[...])
    x0 = u[:, :D_HYENA]
    z = u[:, D_HYENA:2 * D_HYENA] * u[:, 2 * D_HYENA:]
    spec = jnp.dot(fwd_ref[...], z.astype(BF16), preferred_element_type=F32)
    pr, pi = spec[:L], spec[L:]
    kr, ki = kr_ref[...], ki_ref[...]
    row = lax.broadcasted_iota(jnp.int32, pr.shape, 0)
    dc = row == 0
    pk = pi * ki
    yr = pr * kr + jnp.where(dc, 0.0, pk)
    yi = jnp.where(dc, pk, pr * ki - pi * kr)
    yspec = jnp.concatenate([yr, yi], axis=0).astype(BF16)
    y = jnp.dot(inv_ref[...], yspec, preferred_element_type=F32)
    y = y + z * hybias_ref[...]
    y_hy = x0 * y
    gg = ggrp_ref[...]
    out_ref[:, 0:D_HYENA] = (y_hy * _rms(y_hy, D_HYENA) * gg[:, 0:D_HYENA]).astype(BF16)

    us = usc_ref[...].astype(F32)
    y_sc = us[:, :D_SCONV] * _conv3(us[:, D_SCONV:2 * D_SCONV] * us[:, 2 * D_SCONV:],
                                    scconv_ref[...])
    out_ref[:, D_HYENA:D_HYENA + D_SCONV] = (
        y_sc * _rms(y_sc, D_SCONV) * gg[:, D_HYENA:D_HYENA + D_SCONV]).astype(BF16)

    cq = cq_ref[...].astype(F32)
    cqn = (cq * _rms(cq, Q_LORA) * gq_ref[...]).astype(BF16)
    q = jnp.dot(cqn, wq_ref[...], preferred_element_type=F32)
    gqh = gqh_ref[...]
    for h in range(N_HEADS):
        sl = slice(h * HEAD_PAD, (h + 1) * HEAD_PAD)
        qh = q[:, sl]
        qh = qh * _rms(qh, QK_DIM) * gqh
        if rope:
            qh = _rope(qh, cos_ref[...], sin_ref[...])
        q_s[:, sl] = qh.astype(BF16)

    def put_kv(ckv, kpe, r0, n, use_rope):
        ckvn = (ckv * _rms(ckv, KV_LORA) * gkv_ref[...]).astype(BF16)
        kn = jnp.dot(ckvn, wk_ref[...], preferred_element_type=F32)
        v = jnp.dot(ckvn, wv_ref[...], preferred_element_type=F32)
        gkh = gkh_ref[...]
        for h in range(N_HEADS):
            sl = slice(h * HEAD_PAD, (h + 1) * HEAD_PAD)
            kh = kn[:, sl] + kpe
            kh = kh * _rms(kh, QK_DIM) * gkh
            if use_rope:
                kh = _rope(kh, cos_ref[...], sin_ref[...])
            k_s[r0:r0 + n, sl] = kh.astype(BF16)
        lane = lax.broadcasted_iota(jnp.int32, v.shape, 1)
        left = (lane % LANES) < V_DIM
        vl_s[r0:r0 + n, :] = jnp.where(left, v, 0.0).astype(BF16)
        vr_s[r0:r0 + n, :] = jnp.where(left, 0.0, v).astype(BF16)

    put_kv(ckv_ref[...], kpe_ref[...], 0, L, rope)
    if n_ctx:
        put_kv(cckv_ref[...], ckpe_ref[...], L, n_ctx, False)

    def q_block(qi, carry):
        r0 = pl.multiple_of(qi * Q_TILE, Q_TILE)
        for p in range(N_HEADS // 2):
            acc = None
            inv_l = []
            for hh, v_s in ((0, vl_s), (1, vr_s)):
                h = 2 * p + hh
                sl = slice(h * HEAD_PAD, (h + 1) * HEAD_PAD)
                s = lax.dot_general(q_s[pl.ds(r0, Q_TILE), sl], k_s[:, sl],
                                    (((1,), (1,)), ((), ())), preferred_element_type=F32)
                e = jnp.exp(s - jnp.max(s, axis=-1, keepdims=True))
                inv_l.append(1.0 / jnp.sum(e, axis=-1, keepdims=True))
                o = jnp.dot(e.astype(BF16), v_s[:, p * LANES:(p + 1) * LANES],
                            preferred_element_type=F32)
                acc = o if acc is None else acc + o
            lane = lax.broadcasted_iota(jnp.int32, acc.shape, 1)
            yat_s[pl.ds(r0, Q_TILE), p * LANES:(p + 1) * LANES] = acc * jnp.where(
                lane < V_DIM, inv_l[0], inv_l[1])
        return carry

    lax.fori_loop(0, L // Q_TILE, q_block, 0)
    y_at = yat_s[...]
    out_ref[:, D_HYENA + D_SCONV:] = (
        y_at * _rms(y_at, D_MLA) * gg[:, D_HYENA + D_SCONV:]).astype(BF16)


def _mixer_call(l, L, n_seq, seq0, n_ctx, rope, acts, ctx, rope_t, dft, filt, w):
    uhy, usc, cq, ckv, kpe = acts
    seq = lambda c: pl.BlockSpec((L, c), lambda b: (seq0 + b, 0))
    const2 = lambda r, c: _resident((r, c), lambda b: (0, 0))
    layer2 = lambda r, c: _resident((None, r, c), lambda b: (l, 0, 0))
    in_specs = [seq(3 * D_HYENA), seq(3 * D_SCONV), seq(Q_LORA), seq(KV_LORA), seq(LANES)]
    args = [uhy, usc, cq, ckv, kpe]
    if n_ctx:
        in_specs += [pl.BlockSpec((None, None, n_ctx, KV_LORA), lambda b: (b, l, 0, 0)),
                     pl.BlockSpec((None, None, n_ctx, LANES), lambda b: (b, l, 0, 0))]
        args += list(ctx)
    if rope:
        in_specs += [const2(L, HEAD_PAD), const2(L, HEAD_PAD)]
        args += list(rope_t)
    in_specs += [const2(2 * L, L), const2(L, 2 * L), layer2(L, D_HYENA), layer2(L, D_HYENA),
                 layer2(3, 3 * D_HYENA), layer2(1, D_HYENA), layer2(3, D_SCONV),
                 layer2(1, Q_LORA), layer2(Q_LORA, N_HEADS * HEAD_PAD),
                 layer2(1, KV_LORA), layer2(KV_LORA, N_HEADS * HEAD_PAD),
                 layer2(KV_LORA, D_MLA), layer2(1, HEAD_PAD), layer2(1, HEAD_PAD),
                 layer2(1, D_MIX)]
    args += [dft[0], dft[1], filt[0], filt[1], w["hy_conv"], w["hy_bias"], w["sc_conv"],
             w["g_q"], w["wq"], w["g_kv"], w["wk"], w["wv"], w["g_qh"], w["g_kh"], w["g_grp"]]
    lk = L + n_ctx
    return pl.pallas_call(
        functools.partial(_mixer_kernel, L=L, n_ctx=n_ctx, rope=rope),
        grid=(n_seq,),
        in_specs=in_specs,
        out_specs=pl.BlockSpec((L, D_MIX), lambda b: (b, 0)),
        out_shape=jax.ShapeDtypeStruct((n_seq * L, D_MIX), BF16),
        scratch_shapes=[
            pltpu.VMEM((L, N_HEADS * HEAD_PAD), BF16),
            pltpu.VMEM((lk, N_HEADS * HEAD_PAD), BF16),
            pltpu.VMEM((lk, D_MLA), BF16),
            pltpu.VMEM((lk, D_MLA), BF16),
            pltpu.VMEM((L, D_MLA), F32),
        ],
        compiler_params=_params(("parallel",)),
        name=f"mixer_{l}_{L}",
    )(*args)


def _outffn_kernel(x_ref, y_ref, mod_ref, g_ref, wo_ref, wg_ref, wu_ref, wd_ref, o_ref):
    d = D_MODEL
    g1 = mod_ref[:, 2 * d:3 * d]
    sh2 = mod_ref[:, 3 * d:4 * d]
    sc2 = mod_ref[:, 4 * d:5 * d]
    g2 = mod_ref[:, 5 * d:6 * d]
    x = x_ref[...] + g1 * jnp.dot(y_ref[...], wo_ref[...], preferred_element_type=F32)
    h = ((x * _rms(x, d) * g_ref[...]) * (1.0 + sc2) + sh2).astype(BF16)
    gate = jnp.dot(h, wg_ref[...], preferred_element_type=F32)
    up = jnp.dot(h, wu_ref[...], preferred_element_type=F32)
    a = (gate * jax.nn.sigmoid(gate) * up).astype(BF16)
    o_ref[...] = x + g2 * jnp.dot(a, wd_ref[...], preferred_element_type=F32)


def _outffn_call(l, x, y, mod4, g_norm2, w_out, w_gate, w_up, w_down):
    tok = lambda c: pl.BlockSpec((TOK_TILE, c), lambda i: (i, 0))
    return pl.pallas_call(
        _outffn_kernel,
        grid=(N_TOK // TOK_TILE,),
        in_specs=[
            tok(D_MODEL), tok(D_MIX),
            pl.BlockSpec((None, None, 1, 6 * D_MODEL), lambda i: (l, _mod_row(i), 0, 0)),
            pl.BlockSpec((None, 1, D_MODEL), lambda i: (l, 0, 0)),
            _resident((None, D_MIX, D_MODEL), lambda i: (l, 0, 0)),
            _resident((None, D_MODEL, D_FF), lambda i: (l, 0, 0)),
            _resident((None, D_MODEL, D_FF), lambda i: (l, 0, 0)),
            _resident((None, D_FF, D_MODEL), lambda i: (l, 0, 0)),
        ],
        out_specs=tok(D_MODEL),
        out_shape=jax.ShapeDtypeStruct((N_TOK, D_MODEL), F32),
        compiler_params=_params(("parallel",)),
        name=f"outffn_{l}",
    )(x, y, mod4, g_norm2.reshape(DEPTH, 1, D_MODEL), w_out, w_gate, w_up, w_down)


def _pad_heads(w, width):
    k = w.shape[1]
    w = w.reshape(DEPTH, k, N_HEADS, width)
    w = jnp.pad(w, ((0, 0), (0, 0), (0, 0), (0, HEAD_PAD - width)))
    return w.reshape(DEPTH, k, N_HEADS * HEAD_PAD)


def kernel(x_prompt, x_sample, cache_ckv, cache_kpe, c, c_ctx, w_mod, b_mod, g_norm1, w_in,
           hy_conv, hy_w1, hy_b1, hy_freq, hy_w2, hy_b2, hy_w3, hy_bias, sc_conv,
           g_q, w_uq, g_kv, w_ukv, g_qh, g_kh, g_grp, w_out, g_norm2,
           w_ff_gate, w_ff_up, w_ff_down):
    kpe_off = N_IN - QK_ROPE
    w_in_p = jnp.concatenate(
        [w_in[:, :, :kpe_off],
         jnp.pad(w_in[:, :, kpe_off:], ((0, 0), (0, 0), (QK_NOPE, LANES - QK_DIM)))],
        axis=-1).astype(BF16)
    w_ukv4 = w_ukv.reshape(DEPTH, KV_LORA, N_HEADS, QK_NOPE + V_DIM)
    row3 = lambda a: a.reshape(DEPTH, 1, a.shape[-1])
    pad_gain = lambda g: jnp.pad(g, ((0, 0), (0, HEAD_PAD - QK_DIM))).reshape(DEPTH, 1, HEAD_PAD)
    w = {
        "hy_conv": hy_conv, "hy_bias": row3(hy_bias), "sc_conv": sc_conv,
        "g_q": row3(g_q), "g_kv": row3(g_kv), "g_grp": row3(g_grp),
        "wq": _pad_heads(w_uq, QK_DIM).astype(BF16),
        "wk": _pad_heads(w_ukv4[..., :QK_NOPE].reshape(DEPTH, KV_LORA, N_HEADS * QK_NOPE),
                         QK_NOPE).astype(BF16),
        "wv": w_ukv4[..., QK_NOPE:].reshape(DEPTH, KV_LORA, D_MLA).astype(BF16),
        "g_qh": pad_gain(g_qh * (QK_DIM ** -0.5)),
        "g_kh": pad_gain(g_kh),
    }
    w_out_b = w_out.astype(BF16)
    w_gate_b = w_ff_gate.astype(BF16)
    w_up_b = w_ff_up.astype(BF16)
    w_down_b = w_ff_down.astype(BF16)
    pad_k = HY_FF
    hy = (jnp.pad(hy_w1, ((0, 0), (0, LANES - HY_EMB), (0, LANES - HY_FF))),
          jnp.pad(row3(hy_b1), ((0, 0), (0, 0), (0, LANES - HY_FF))),
          jnp.pad(row3(hy_freq), ((0, 0), (0, 0), (0, LANES - HY_FF))),
          jnp.pad(hy_w2, ((0, 0), (0, LANES - pad_k), (0, LANES - HY_FF))),
          jnp.pad(row3(hy_b2), ((0, 0), (0, 0), (0, LANES - HY_FF))),
          jnp.pad(hy_w3, ((0, 0), (0, LANES - pad_k), (0, 0))))
    ctx = (cache_ckv, jnp.pad(cache_kpe, ((0, 0), (0, 0), (0, 0), (QK_NOPE, LANES - QK_DIM))))

    dft, filt = {}, {}
    for L in (SEQ, DEC_SEQ):
        cos_t, sin_t, fwd, inv = _dft_tables(L)
        dft[L] = (fwd, inv)
        z, win = _filter_consts(L)
        filt[L] = _filter_call(L, z, win, cos_t, sin_t, *hy)
    rope_t = _rope_tables(DEC_SEQ)

    cs = jnp.concatenate([c_ctx[None, :], c,
                          jnp.zeros((MOD_ROWS - 1 - DEC_BATCH, D_MODEL), F32)], axis=0)
    mod4 = _mod_call(cs, w_mod, b_mod).reshape(DEPTH, MOD_ROWS, 1, 6 * D_MODEL)

    x = jnp.concatenate([x_prompt.reshape(N_CTX_TOK, D_MODEL),
                         x_sample.reshape(N_DEC_TOK, D_MODEL)], axis=0)
    ckv_l, kpe_l = [], []
    for l in range(DEPTH):
        acts = _inproj_call(l, x, mod4, g_norm1, w_in_p)
        y_ctx = _mixer_call(l, SEQ, BATCH, 0, 0, False, acts, None, None, dft[SEQ], filt[SEQ], w)
        y_dec = _mixer_call(l, DEC_SEQ, DEC_BATCH, N_CTX_TOK // DEC_SEQ, PAST_LEN, True, acts,
                            ctx, rope_t, dft[DEC_SEQ], filt[DEC_SEQ], w)
        y = jnp.concatenate([y_ctx, y_dec], axis=0)
        x = _outffn_call(l, x, y, mod4, g_norm2, w_out_b, w_gate_b, w_up_b, w_down_b)
        ckv_l.append(acts[3][:N_CTX_TOK].reshape(BATCH, SEQ, KV_LORA))
        kpe_l.append(acts[4][:N_CTX_TOK, QK_NOPE:QK_DIM].reshape(BATCH, SEQ, QK_ROPE))

    y_prompt = x[:N_CTX_TOK].reshape(BATCH, SEQ, D_MODEL)
    y_sample = x[N_CTX_TOK:].reshape(DEC_BATCH, DEC_SEQ, D_MODEL)
    return (y_prompt, y_sample, jnp.stack(ckv_l, axis=1), jnp.stack(kpe_l, axis=1))
```
